```python
import math
import jax, jax.numpy as jnp
from jax import lax
import numpy as np

D_MODEL = 1024
BATCH = 4
SEQ = 4096
DEPTH = 1

CONV_DIM = 512
CONV_GROUPS = 8
CONV_K = 3
HGRN_HEADS = 4
HGRN_DK = 128
HGRN_DV = 128
HGRN_FDIM = HGRN_HEADS * HGRN_DK
HGRN_VDIM = HGRN_HEADS * HGRN_DV
CHUNK = 32
N_BRANCH = 2
IN_COL_SIZES = (CONV_DIM, CONV_DIM, CONV_DIM,
                HGRN_FDIM, HGRN_FDIM, HGRN_FDIM, HGRN_VDIM, HGRN_VDIM,
                D_MODEL, D_MODEL)
IN_COLS = 3 * CONV_DIM + 3 * HGRN_FDIM + 2 * HGRN_VDIM + N_BRANCH * D_MODEL
N_EXPERTS = 32
TOP_K = 4
D_FF = D_MODEL
SWIGLU_ALPHA = 1.702
SWIGLU_LIMIT = 7.0
MOE_BLOCK = 256
DN_ALPHA = (2.0 * DEPTH) ** 0.25
DN_BETA = (8.0 * DEPTH) ** -0.25
LN_EPS = 1e-5
RMS_EPS = 1e-6

kernel_name = "hybrid_conv_hgrn2_moe_deepnorm_encoder"


def layer_norm(x, g, b):
    xf = x.astype(jnp.float32)
    mu = jnp.mean(xf, axis=-1, keepdims=True)
    xc = xf - mu
    var = jnp.mean(xc * xc, axis=-1, keepdims=True)
    return (xc * lax.rsqrt(var + LN_EPS) * g.astype(jnp.float32) + b.astype(jnp.float32)).astype(x.dtype)


def split_cols(proj):
    outs, start = [], 0
    for size in IN_COL_SIZES:
        outs.append(proj[..., start:start + size])
        start += size
    return outs


def centred_short_conv(u, w):
    L = u.shape[1]
    half = CONV_K // 2
    up = jnp.pad(u, ((0, 0), (half, half), (0, 0)))
    return sum(w[j] * up[:, j:j + L] for j in range(CONV_K))


def hgrn2_chunked(q, k, v, logf):
    Bsz, L, H, dk = q.shape
    dv = v.shape[-1]
    n = L // CHUNK

    def to_chunks(t):
        return t.reshape(Bsz, n, CHUNK, H, t.shape[-1]).transpose(0, 3, 1, 2, 4)

    q, k, v, logf = to_chunks(q), to_chunks(k), to_chunks(v), to_chunks(logf)
    b = jnp.cumsum(logf, axis=3)
    b_ref = b[:, :, :, CHUNK // 2 - 1:CHUNK // 2]
    b_last = b[:, :, :, CHUNK - 1:CHUNK]
    scores = jnp.einsum('bhntd,bhnsd->bhnts', q * jnp.exp(b - b_ref), k * jnp.exp(b_ref - b))
    causal_in_dir = jnp.tril(jnp.ones((CHUNK, CHUNK), dtype=bool))
    scores = jnp.where(causal_in_dir, scores, 0.0)
    o_intra = jnp.einsum('bhnts,bhnse->bhnte', scores, v)
    kv = jnp.einsum('bhnsd,bhnse->bhnde', k * jnp.exp(b_last - b), v)
    decay = jnp.exp(b_last[:, :, :, 0])

    def step(S, inp):
        kv_c, dec_c = inp
        return dec_c[..., None] * S + kv_c, S

    S0 = jnp.zeros((Bsz, H, dk, dv), jnp.float32)
    _, S_prev = lax.scan(step, S0, (kv.transpose(2, 0, 1, 3, 4), decay.transpose(2, 0, 1, 3)))
    S_prev = S_prev.transpose(1, 2, 0, 3, 4)
    o_inter = jnp.einsum('bhntd,bhnde->bhnte', q * jnp.exp(b), S_prev)
    o = o_intra + o_inter
    return o.transpose(0, 2, 3, 1, 4).reshape(Bsz, L, H, dv)


def hgrn2_gate(f_raw, lb):
    f = lb + (1.0 - lb) * jax.nn.sigmoid(f_raw.astype(jnp.float32))
    return jnp.log(f), 1.0 - f


def token_mixer(u, w_in, conv_w, p_a, lb_fwd, lb_bwd, norm_w, p_b, w_o):
    Bsz, L, _ = u.shape
    proj = u @ w_in
    c_b, c_c, c_h, q_raw, f_fw, f_bw, i_raw, g_raw, gate_a, gate_b = split_cols(proj)
    y_a = c_b * centred_short_conv(c_c * c_h, conv_w)
    heads = lambda t: t.reshape(Bsz, L, HGRN_HEADS, -1)
    q = heads(jax.nn.silu(q_raw.astype(jnp.float32)))
    v = heads(i_raw.astype(jnp.float32))
    logf_f, k_f = hgrn2_gate(f_fw, lb_fwd)
    logf_b, k_b = hgrn2_gate(f_bw, lb_bwd)
    o_fwd = hgrn2_chunked(q, heads(k_f), v, heads(logf_f))
    flip = lambda t: jnp.flip(t, axis=1)
    o_bwd = flip(hgrn2_chunked(flip(q), flip(heads(k_b)), flip(v), flip(heads(logf_b))))
    o = o_fwd + o_bwd
    o = o * lax.rsqrt(jnp.mean(o * o, axis=-1, keepdims=True) + RMS_EPS)
    o = o * norm_w.astype(jnp.float32).reshape(HGRN_HEADS, HGRN_DV)
    y_b = (o.reshape(Bsz, L, HGRN_VDIM) * jax.nn.silu(g_raw.astype(jnp.float32))).astype(u.dtype)
    mix = jax.nn.sigmoid(gate_a) * (y_a @ p_a) + jax.nn.sigmoid(gate_b) * (y_b @ p_b)
    return mix @ w_o


def clamped_swiglu_expert(xblk, w1e, b1e, w2e, b2e):
    h = xblk @ w1e + b1e
    h_glu = jnp.minimum(h[:, :D_FF], SWIGLU_LIMIT)
    h_lin = jnp.clip(h[:, D_FF:], -SWIGLU_LIMIT, SWIGLU_LIMIT)
    a = h_glu * jax.nn.sigmoid(SWIGLU_ALPHA * h_glu) * (h_lin + 1.0)
    return a @ w2e + b2e


def moe_ffn(x, w_router, b_router, w1, b1, w2, b2):
    Bsz, L, D = x.shape
    xt = x.reshape(-1, D)
    T = xt.shape[0]
    logits = (xt @ w_router + b_router).astype(jnp.float32)
    top_vals, top_idx = lax.top_k(logits, TOP_K)
    gates = jax.nn.softmax(top_vals, axis=-1)
    N = T * TOP_K
    e_flat = top_idx.reshape(-1).astype(jnp.int32)
    tok_flat = jnp.arange(N, dtype=jnp.int32) // TOP_K
    g_flat = gates.reshape(-1)
    order = jnp.argsort(e_flat)
    e_sorted = e_flat[order]
    counts = jnp.bincount(e_flat, length=N_EXPERTS).astype(jnp.int32)
    padded = ((counts + MOE_BLOCK - 1) // MOE_BLOCK) * MOE_BLOCK
    start = jnp.cumsum(counts) - counts
    ends_p = jnp.cumsum(padded)
    pstart = ends_p - padded
    dest = pstart[e_sorted] + (jnp.arange(N, dtype=jnp.int32) - start[e_sorted])
    n_blocks = -(-N // MOE_BLOCK) + N_EXPERTS
    P = n_blocks * MOE_BLOCK
    buf_tok = jnp.zeros((P,), jnp.int32).at[dest].set(tok_flat[order])
    buf_gate = jnp.zeros((P,), jnp.float32).at[dest].set(g_flat[order])
    block_start = jnp.arange(n_blocks, dtype=jnp.int32) * MOE_BLOCK
    block_expert = jnp.clip(jnp.searchsorted(ends_p, block_start, side='right'), 0, N_EXPERTS - 1)
    xb = xt[buf_tok].reshape(n_blocks, MOE_BLOCK, D)

    def run_block(args):
        xblk, e = args
        return clamped_swiglu_expert(xblk, w1[e], b1[e], w2[e], b2[e])

    yb = lax.map(run_block, (xb, block_expert)).reshape(P, D)
    yb = yb * buf_gate[:, None].astype(yb.dtype)
    out = jax.ops.segment_sum(yb, buf_tok, num_segments=T)
    return out.reshape(Bsz, L, D).astype(x.dtype)


def setup_inputs(seed: int = 0) -> dict:
    key = jax.random.key(seed)
    ks = jax.random.split(key, 20)
    f32 = jnp.float32

    def nrm(k, shape, s):
        return jax.random.normal(k, shape, f32) * s

    return {
        "x": nrm(ks[0], (BATCH, SEQ, D_MODEL), 1.0),
        "w_in": nrm(ks[1], (DEPTH, D_MODEL, IN_COLS), D_MODEL ** -0.5),
        "conv_w": nrm(ks[2], (DEPTH, CONV_K, CONV_DIM), CONV_K ** -0.5),
        "p_a": nrm(ks[3], (DEPTH, CONV_DIM, D_MODEL), CONV_DIM ** -0.5),
        "hgrn_lb_logits": nrm(ks[4], (2, DEPTH + 1, HGRN_FDIM), 0.3),
        "hgrn_norm_w": 1.0 + nrm(ks[5], (DEPTH, HGRN_VDIM), 0.02),
        "p_b": nrm(ks[6], (DEPTH, HGRN_VDIM, D_MODEL), HGRN_VDIM ** -0.5),
        "w_o": nrm(ks[7], (DEPTH, D_MODEL, D_MODEL), D_MODEL ** -0.5 * DN_BETA),
        "ln1_g": 1.0 + nrm(ks[8], (DEPTH, D_MODEL), 0.02),
        "ln1_b": nrm(ks[9], (DEPTH, D_MODEL), 0.02),
        "w_router": nrm(ks[10], (DEPTH, D_MODEL, N_EXPERTS), D_MODEL ** -0.5),
        "b_router": nrm(ks[11], (DEPTH, N_EXPERTS), 0.01),
        "w1": nrm(ks[12], (DEPTH, N_EXPERTS, D_MODEL, 2 * D_FF), D_MODEL ** -0.5),
        "b1": nrm(ks[13], (DEPTH, N_EXPERTS, 2 * D_FF), 0.01),
        "w2": nrm(ks[14], (DEPTH, N_EXPERTS, D_FF, D_MODEL), D_FF ** -0.5 * DN_BETA),
        "b2": nrm(ks[15], (DEPTH, N_EXPERTS, D_MODEL), 0.01),
        "ln2_g": 1.0 + nrm(ks[16], (DEPTH, D_MODEL), 0.02),
        "ln2_b": nrm(ks[17], (DEPTH, D_MODEL), 0.02),
    }


def reference(x, w_in, conv_w, p_a, hgrn_lb_logits, hgrn_norm_w, p_b, w_o, ln1_g, ln1_b,
              w_router, b_router, w1, b1, w2, b2, ln2_g, ln2_b):
    lb_table = jnp.cumsum(jax.nn.softmax(hgrn_lb_logits.astype(jnp.float32), axis=1), axis=1)
    h = x
    for l in range(DEPTH):
        mixed = token_mixer(h, w_in[l], conv_w[l], p_a[l], lb_table[0, l], lb_table[1, l],
                            hgrn_norm_w[l], p_b[l], w_o[l])
        h = layer_norm(DN_ALPHA * h + mixed, ln1_g[l], ln1_b[l])
        ffn = moe_ffn(h, w_router[l], b_router[l], w1[l], b1[l], w2[l], b2[l])
        h = layer_norm(DN_ALPHA * h + ffn, ln2_g[l], ln2_b[l])
    return h
```

```python
import functools

import jax
import jax.numpy as jnp
from jax import lax
from jax.experimental import pallas as pl
from jax.experimental.pallas import tpu as pltpu

F32 = jnp.float32
BF16 = jnp.bfloat16
I32 = jnp.int32
U32 = jnp.uint32

D_MODEL = 1024
CONV_DIM = 512
HEADS = 4
HEAD_DIM = 128
HDIM = HEADS * HEAD_DIM
CHUNK = 32
N_EXPERTS = 32
TOP_K = 4
D_FF = 1024
SWIGLU_ALPHA = 1.702
SWIGLU_LIMIT = 7.0
DN_ALPHA = 2.0 ** 0.25
LN_EPS = 1e-5
RMS_EPS = 1e-6

C_CB, C_CC, C_CH, C_Q, C_FF, C_FB, C_I, C_G, C_GA, C_GB = (
    0, 512, 1024, 1536, 2048, 2560, 3072, 3584, 4096, 5120)
IN_COLS = 6144

TM_IN = 256
TM_MIX = 512
TM_DISP = 512
TM_COMB = 256
MOE_BM = 256
VMEM_LIMIT = 56 * 1024 * 1024


def _sigmoid(x):
    return 1.0 / (1.0 + jnp.exp(-x))


def _pack_bf16_pairs(x):
    n = x.shape[1] // 2
    hi = lax.bitcast_convert_type(x[:, :n].astype(BF16).astype(F32), U32)
    lo = lax.bitcast_convert_type(x[:, n:].astype(BF16).astype(F32), U32)
    return hi | (lo >> 16)


def _unpack_bf16_pairs(p):
    hi = lax.bitcast_convert_type(p & jnp.uint32(0xFFFF0000), F32)
    lo = lax.bitcast_convert_type(p << 16, F32)
    return jnp.concatenate([hi, lo], axis=1)


def _chunk_cumsum(x, reverse):
    row = lax.broadcasted_iota(I32, x.shape, 0)
    s = 1
    while s < CHUNK:
        if reverse:
            y = pltpu.roll(x, CHUNK - s, 0)
            x = x + jnp.where(row < CHUNK - s, y, 0.0)
        else:
            y = pltpu.roll(x, s, 0)
            x = x + jnp.where(row >= s, y, 0.0)
        s *= 2
    return x


def _inproj_kernel(x_ref, w_ref, lbl_ref,
                   z_ref, cb_ref, v_ref, g_ref, ga_ref, gb_ref,
                   qhf_ref, khf_ref, qtf_ref, ktf_ref, decf_ref,
                   qhb_ref, khb_ref, qtb_ref, ktb_ref, decb_ref):
    x = x_ref[...].astype(BF16)

    def proj(c0, n):
        return jnp.dot(x, w_ref[:, c0:c0 + n], preferred_element_type=F32)

    cb_ref[...] = proj(C_CB, CONV_DIM).astype(BF16)
    z_ref[...] = (proj(C_CC, CONV_DIM) * proj(C_CH, CONV_DIM)).astype(BF16)
    v_ref[...] = proj(C_I, HDIM).astype(BF16)
    g_raw = proj(C_G, HDIM)
    g_ref[...] = (g_raw * _sigmoid(g_raw)).astype(BF16)
    ga_ref[...] = _sigmoid(proj(C_GA, D_MODEL)).astype(BF16)
    gb_ref[...] = _sigmoid(proj(C_GB, D_MODEL)).astype(BF16)
    q_raw = proj(C_Q, HDIM)
    q = q_raw * _sigmoid(q_raw)

    lbl = lbl_ref[...]
    n_chunks = x_ref.shape[0] // CHUNK
    for d, (c0, qh_ref, kh_ref, qt_ref, kt_ref, dec_ref) in enumerate((
            (C_FF, qhf_ref, khf_ref, qtf_ref, ktf_ref, decf_ref),
            (C_FB, qhb_ref, khb_ref, qtb_ref, ktb_ref, decb_ref))):
        reverse = d == 1
        l0 = lbl[2 * d:2 * d + 1, :]
        l1 = lbl[2 * d + 1:2 * d + 2, :]
        m = jnp.maximum(l0, l1)
        e0 = jnp.exp(l0 - m)
        lb = e0 / (e0 + jnp.exp(l1 - m))
        f = lb + (1.0 - lb) * _sigmoid(proj(c0, HDIM))
        logf = jnp.log(f)
        k = 1.0 - f
        i_ref, i_last = (CHUNK // 2, 0) if reverse else (CHUNK // 2 - 1, CHUNK - 1)
        for c in range(n_chunks):
            r = slice(c * CHUNK, (c + 1) * CHUNK)
            b = _chunk_cumsum(logf[r], reverse)
            b_ref = b[i_ref:i_ref + 1, :]
            b_last = b[i_last:i_last + 1, :]
            e = jnp.exp(b - b_ref)
            qh = q[r] * e
            kh = k[r] / e
            qh_ref[r, :] = qh.astype(BF16)
            kh_ref[r, :] = kh.astype(BF16)
            qt_ref[r, :] = (qh * jnp.exp(b_ref)).astype(BF16)
            kt_ref[r, :] = (kh * jnp.exp(b_last - b_ref)).astype(BF16)
            dec_ref[c:c + 1, :] = jnp.exp(b_last)


def _inproj(x2, w_bf, lbl):
    t = x2.shape[0]
    tm = TM_IN
    tok = lambda n, dt: jax.ShapeDtypeStruct((t, n), dt)
    dec = jax.ShapeDtypeStruct((t // CHUNK, HDIM), F32)
    tok_spec = lambda n: pl.BlockSpec((tm, n), lambda i: (i, 0))
    dec_spec = pl.BlockSpec((tm // CHUNK, HDIM), lambda i: (i, 0))
    dir_shapes = [tok(HDIM, BF16)] * 4 + [dec]
    dir_specs = [tok_spec(HDIM)] * 4 + [dec_spec]
    return pl.pallas_call(
        _inproj_kernel,
        grid=(t // tm,),
        in_specs=[
            pl.BlockSpec((tm, D_MODEL), lambda i: (i, 0)),
            pl.BlockSpec((D_MODEL, IN_COLS), lambda i: (0, 0), pipeline_mode=pl.Buffered(1)),
            pl.BlockSpec((4, HDIM), lambda i: (0, 0)),
        ],
        out_specs=[tok_spec(CONV_DIM), tok_spec(CONV_DIM), tok_spec(HDIM), tok_spec(HDIM),
                   tok_spec(D_MODEL), tok_spec(D_MODEL)] + dir_specs + dir_specs,
        out_shape=[tok(CONV_DIM, BF16), tok(CONV_DIM, BF16), tok(HDIM, BF16), tok(HDIM, BF16),
                   tok(D_MODEL, BF16), tok(D_MODEL, BF16)] + dir_shapes + dir_shapes,
        compiler_params=pltpu.CompilerParams(
            dimension_semantics=("arbitrary",), vmem_limit_bytes=VMEM_LIMIT),
        name="inproj",
    )(x2, w_bf, lbl)


def _hgrn_kernel(qhf_ref, khf_ref, qtf_ref, ktf_ref, decf_ref,
                 qhb_ref, khb_ref, qtb_ref, ktb_ref, decb_ref,
                 v_ref, g_ref, nw_ref, y_ref, o_scr):
    n_chunks = v_ref.shape[0] // CHUNK
    half = n_chunks // 2
    row = lax.broadcasted_iota(I32, (CHUNK, CHUNK), 0)
    col = lax.broadcasted_iota(I32, (CHUNK, CHUNK), 1)
    mask_f = col <= row
    mask_b = col >= row
    nt = (((1,), (1,)), ((), ()))
    tn = (((0,), (0,)), ((), ()))

    def chunk(refs, mask, c, s_t):
        qh_ref, kh_ref, qt_ref, kt_ref, dec_ref = refs
        r = pl.ds(pl.multiple_of(c * CHUNK, CHUNK), CHUNK)
        v = v_ref[r, :]
        sc = lax.dot_general(qh_ref[r, :], kh_ref[r, :], nt, preferred_element_type=F32)
        sc = jnp.where(mask, sc, 0.0)
        o = jnp.dot(sc.astype(BF16), v, preferred_element_type=F32)
        o = o + lax.dot_general(qt_ref[r, :], s_t.astype(BF16), nt, preferred_element_type=F32)
        kv_t = lax.dot_general(v, kt_ref[r, :], tn, preferred_element_type=F32)
        s_new = s_t * dec_ref[pl.ds(c, 1), :] + kv_t
        return r, o, s_new

    fwd = (qhf_ref, khf_ref, qtf_ref, ktf_ref, decf_ref)
    bwd = (qhb_ref, khb_ref, qtb_ref, ktb_ref, decb_ref)

    def first_half(i, carry):
        s_f, s_b = carry
        r_f, o_f, s_f = chunk(fwd, mask_f, i, s_f)
        r_b, o_b, s_b = chunk(bwd, mask_b, n_chunks - 1 - i, s_b)
        o_scr[r_f, :] = o_f
        o_scr[r_b, :] = o_b
        return s_f, s_b

    def finish(r, o):
        o = o + o_scr[r, :]
        o = o * lax.rsqrt(jnp.mean(o * o, axis=-1, keepdims=True) + RMS_EPS)
        y_ref[r, :] = (o * nw_ref[...] * g_ref[r, :].astype(F32)).astype(BF16)

    def second_half(i, carry):
        s_f, s_b = carry
        r_f, o_f, s_f = chunk(fwd, mask_f, i, s_f)
        r_b, o_b, s_b = chunk(bwd, mask_b, n_chunks - 1 - i, s_b)
        finish(r_f, o_f)
        finish(r_b, o_b)
        return s_f, s_b

    zero = jnp.zeros((HEAD_DIM, HEAD_DIM), F32)
    carry = lax.fori_loop(0, half, first_half, (zero, zero))
    lax.fori_loop(half, n_chunks, second_half, carry)


def _hgrn(dirs_f, dirs_b, v, g, norm_w, batch, seq):
    t = v.shape[0]
    n_chunks = seq // CHUNK
    tok_spec = pl.BlockSpec((seq, HEAD_DIM), lambda b, h: (b, h))
    dec_spec = pl.BlockSpec((n_chunks, HEAD_DIM), lambda b, h: (b, h))
    dir_specs = [tok_spec] * 4 + [dec_spec]
    return pl.pallas_call(
        _hgrn_kernel,
        grid=(batch, HEADS),
        in_specs=dir_specs + dir_specs + [tok_spec, tok_spec,
                                          pl.BlockSpec((1, HEAD_DIM), lambda b, h: (0, h))],
        out_specs=tok_spec,
        out_shape=jax.ShapeDtypeStruct((t, HDIM), BF16),
        scratch_shapes=[pltpu.VMEM((seq, HEAD_DIM), F32)],
        compiler_params=pltpu.CompilerParams(
            dimension_semantics=("arbitrary", "arbitrary"), vmem_limit_bytes=VMEM_LIMIT),
        name="hgrn",
    )(*dirs_f, *dirs_b, v, g, norm_w)


def _mix_kernel(seq_tiles,
                z_ref, zp_ref, zn_ref, cb_ref, yb_ref, ga_ref, gb_ref, x_ref,
                cw_ref, pa_ref, pb_ref, wo_ref, g1_ref, b1_ref, wr_ref, br_ref,
                h_ref, hp_ref, er_ref, gt_ref, cnt_ref, run_scr):
    i = pl.program_id(0)
    tm = z_ref.shape[0]

    @pl.when(i == 0)
    def _():
        run_scr[...] = jnp.zeros_like(run_scr)

    z = z_ref[...].astype(F32)
    pos = i % seq_tiles
    prev_row = jnp.where(pos == 0, 0.0, zp_ref[15:16, :].astype(F32))
    next_row = jnp.where(pos == seq_tiles - 1, 0.0, zn_ref[0:1, :].astype(F32))
    row = lax.broadcasted_iota(I32, z.shape, 0)
    z_dn = jnp.where(row == 0, prev_row, pltpu.roll(z, 1, 0))
    z_up = jnp.where(row == tm - 1, next_row, pltpu.roll(z, tm - 1, 0))
    cw = cw_ref[...]
    conv = cw[0:1, :] * z_dn + cw[1:2, :] * z + cw[2:3, :] * z_up
    y_a = (cb_ref[...].astype(F32) * conv).astype(BF16)

    a = jnp.dot(y_a, pa_ref[...], preferred_element_type=F32)
    b = jnp.dot(yb_ref[...], pb_ref[...], preferred_element_type=F32)
    mix = ga_ref[...].astype(F32) * a + gb_ref[...].astype(F32) * b
    out = jnp.dot(mix.astype(BF16), wo_ref[...], preferred_element_type=F32)

    r = DN_ALPHA * x_ref[...] + out
    mu = jnp.mean(r, axis=-1, keepdims=True)
    rc = r - mu
    var = jnp.mean(rc * rc, axis=-1, keepdims=True)
    h = rc * lax.rsqrt(var + LN_EPS) * g1_ref[...] + b1_ref[...]
    h_ref[...] = h
    hp_ref[...] = _pack_bf16_pairs(h)

    logits = lax.dot_general(wr_ref[...], h.astype(BF16), (((1,), (1,)), ((), ())),
                             preferred_element_type=F32) + br_ref[...]
    eidx = lax.broadcasted_iota(I32, logits.shape, 0)
    onehots, vals, ids = [], [], []
    l = logits
    for _ in range(TOP_K):
        m = jnp.max(l, axis=0, keepdims=True)
        idx = jnp.min(jnp.where(l == m, eidx, N_EXPERTS), axis=0, keepdims=True)
        oh = eidx == idx
        onehots.append(oh)
        vals.append(m)
        ids.append(idx)
        l = jnp.where(oh, -jnp.inf, l)
    ws = [jnp.exp(v - vals[0]) for v in vals]
    wsum = ws[0] + ws[1] + ws[2] + ws[3]
    for k in range(TOP_K):
        gt_ref[k:k + 1, :] = ws[k] / wsum
    gt_ref[TOP_K:, :] = jnp.zeros((8 - TOP_K, tm), F32)

    assigned = (onehots[0] | onehots[1] | onehots[2] | onehots[3])
    a_f = jnp.where(assigned, 1.0, 0.0)
    srow = lax.broadcasted_iota(I32, (tm, tm), 0)
    scol = lax.broadcasted_iota(I32, (tm, tm), 1)
    strict_upper = jnp.where(srow < scol, 1.0, 0.0).astype(BF16)
    before = jnp.dot(a_f.astype(BF16), strict_upper, preferred_element_type=F32) + run_scr[:, 0:1]
    for k in range(TOP_K):
        rank = jnp.sum(jnp.where(onehots[k], before, 0.0), axis=0, keepdims=True)
        er_ref[k:k + 1, :] = ids[k]
        er_ref[TOP_K + k:TOP_K + k + 1, :] = rank.astype(I32)
    run_new = run_scr[...] + jnp.sum(a_f, axis=1, keepdims=True)
    run_scr[...] = run_new
    cnt_ref[...] = run_new


def _mix(z, cb, yb, ga, gb, x2, conv_w, pa, pb, wo, g1, b1, wr_t, br, seq):
    t = z.shape[0]
    tm = TM_MIX
    n16 = tm // 16
    tok = lambda n: pl.BlockSpec((tm, n), lambda i: (i, 0))
    full = lambda a: pl.BlockSpec(a.shape, lambda i: (0,) * a.ndim)
    return pl.pallas_call(
        functools.partial(_mix_kernel, seq // tm),
        grid=(t // tm,),
        in_specs=[
            tok(CONV_DIM),
            pl.BlockSpec((16, CONV_DIM), lambda i: (jnp.maximum(i * n16 - 1, 0), 0)),
            pl.BlockSpec((16, CONV_DIM), lambda i: (jnp.minimum((i + 1) * n16, t // 16 - 1), 0)),
            tok(CONV_DIM), tok(HDIM), tok(D_MODEL), tok(D_MODEL), tok(D_MODEL),
            full(conv_w), full(pa), full(pb), full(wo), full(g1), full(b1), full(wr_t), full(br),
        ],
        out_specs=[
            tok(D_MODEL), tok(D_MODEL // 2),
            pl.BlockSpec((8, tm), lambda i: (0, i)),
            pl.BlockSpec((8, tm), lambda i: (0, i)),
            pl.BlockSpec((N_EXPERTS, 128), lambda i: (0, 0)),
        ],
        out_shape=[
            jax.ShapeDtypeStruct((t, D_MODEL), F32),
            jax.ShapeDtypeStruct((t, D_MODEL // 2), U32),
            jax.ShapeDtypeStruct((8, t), I32),
            jax.ShapeDtypeStruct((8, t), F32),
            jax.ShapeDtypeStruct((N_EXPERTS, 128), F32),
        ],
        scratch_shapes=[pltpu.VMEM((N_EXPERTS, 128), F32)],
        compiler_params=pltpu.CompilerParams(
            dimension_semantics=("arbitrary",), vmem_limit_bytes=VMEM_LIMIT),
        name="mix",
    )(z, z, z, cb, yb, ga, gb, x2, conv_w, pa, pb, wo, g1, b1, wr_t, br)


def _row_copy(src_ref, src_row, dst_ref, dst_row, sem):
    return pltpu.make_async_copy(src_ref.at[pl.ds(src_row, 1)], dst_ref.at[pl.ds(dst_row, 1)], sem)


def _dispatch_kernel(slot_ref, hp_ref, xb_in_ref, xb_ref, sem):
    del xb_in_ref
    tm = hp_ref.shape[0]

    def start(t, c):
        for k in range(TOP_K):
            _row_copy(hp_ref, t, xb_ref, slot_ref[k, t], sem).start()
        return c

    def wait(t, c):
        for k in range(TOP_K):
            _row_copy(hp_ref, t, xb_ref, slot_ref[k, t], sem).wait()
        return c

    lax.fori_loop(0, tm, start, 0)
    lax.fori_loop(0, tm, wait, 0)


def _dispatch(slots, hp, n_rows):
    t = hp.shape[0]
    tm = TM_DISP
    xb0 = jnp.zeros((n_rows, D_MODEL // 2), U32)
    return pl.pallas_call(
        _dispatch_kernel,
        grid=(t // tm,),
        in_specs=[
            pl.BlockSpec((TOP_K, tm), lambda i: (0, i), memory_space=pltpu.SMEM),
            pl.BlockSpec((tm, D_MODEL // 2), lambda i: (i, 0)),
            pl.BlockSpec(memory_space=pl.ANY),
        ],
        out_specs=pl.BlockSpec(memory_space=pl.ANY),
        out_shape=jax.ShapeDtypeStruct((n_rows, D_MODEL // 2), U32),
        scratch_shapes=[pltpu.SemaphoreType.DMA],
        input_output_aliases={2: 0},
        compiler_params=pltpu.CompilerParams(
            dimension_semantics=("arbitrary",), vmem_limit_bytes=VMEM_LIMIT),
        name="dispatch",
    )(slots, hp, xb0)


def _moe_kernel(be_ref, nb_ref, xb_ref, w1_ref, b1_ref, w2_ref, b2_ref, yb_ref, w1b_scr, w2b_scr):
    i = pl.program_id(0)
    prev = be_ref[jnp.maximum(i - 1, 0)]
    valid = i < nb_ref[0]

    @pl.when(valid & ((i == 0) | (be_ref[i] != prev)))
    def _():
        w1b_scr[...] = w1_ref[0].astype(BF16)
        w2b_scr[...] = w2_ref[0].astype(BF16)

    @pl.when(valid)
    def _():
        x = _unpack_bf16_pairs(xb_ref[...]).astype(BF16)
        h = jnp.dot(x, w1b_scr[...], preferred_element_type=F32) + b1_ref[0]
        h_glu = jnp.minimum(h[:, :D_FF], SWIGLU_LIMIT)
        h_lin = jnp.clip(h[:, D_FF:], -SWIGLU_LIMIT, SWIGLU_LIMIT)
        a = h_glu * _sigmoid(SWIGLU_ALPHA * h_glu) * (h_lin + 1.0)
        y = jnp.dot(a.astype(BF16), w2b_scr[...], preferred_element_type=F32) + b2_ref[0]
        yb_ref[...] = _pack_bf16_pairs(y)

    @pl.when(jnp.logical_not(valid))
    def _():
        yb_ref[...] = jnp.zeros_like(yb_ref)


def _moe(block_expert, n_valid, xb, w1, b1, w2, b2):
    n_rows = xb.shape[0]
    n_blocks = n_rows // MOE_BM
    blk = lambda i, be, nb: (jnp.minimum(i, nb[0] - 1), 0)
    exp3 = lambda i, be, nb: (be[i], 0, 0)
    return pl.pallas_call(
        _moe_kernel,
        grid_spec=pltpu.PrefetchScalarGridSpec(
            num_scalar_prefetch=2,
            grid=(n_blocks,),
            in_specs=[
                pl.BlockSpec((MOE_BM, D_MODEL // 2), blk),
                pl.BlockSpec((1, D_MODEL, 2 * D_FF), exp3),
                pl.BlockSpec((1, 1, 2 * D_FF), exp3),
                pl.BlockSpec((1, D_FF, D_MODEL), exp3),
                pl.BlockSpec((1, 1, D_MODEL), exp3),
            ],
            out_specs=pl.BlockSpec((MOE_BM, D_MODEL // 2), lambda i, be, nb: (i, 0)),
            scratch_shapes=[pltpu.VMEM((D_MODEL, 2 * D_FF), BF16), pltpu.VMEM((D_FF, D_MODEL), BF16)],
        ),
        out_shape=jax.ShapeDtypeStruct((n_rows, D_MODEL // 2), U32),
        compiler_params=pltpu.CompilerParams(
            dimension_semantics=("arbitrary",), vmem_limit_bytes=VMEM_LIMIT),
        name="moe",
    )(block_expert, n_valid, xb, w1, b1, w2, b2)


def _combine_kernel(slot_ref, yb_ref, gt_ref, h_ref, g2_ref, b2_ref, o_ref, buf, sem):
    tm = h_ref.shape[0]

    def start(t, c):
        for k in range(TOP_K):
            _row_copy(yb_ref, slot_ref[k, t], buf.at[k], t, sem).start()
        return c

    def wait(t, c):
        for k in range(TOP_K):
            _row_copy(yb_ref, slot_ref[k, t], buf.at[k], t, sem).wait()
        return c

    lax.fori_loop(0, tm, start, 0)
    lax.fori_loop(0, tm, wait, 0)

    gt = gt_ref[...]
    ffn = jnp.zeros((tm, D_MODEL), F32)
    for k in range(TOP_K):
        ffn = ffn + gt[:, k:k + 1] * _unpack_bf16_pairs(buf[k])
    r = DN_ALPHA * h_ref[...] + ffn
    mu = jnp.mean(r, axis=-1, keepdims=True)
    rc = r - mu
    var = jnp.mean(rc * rc, axis=-1, keepdims=True)
    o_ref[...] = rc * lax.rsqrt(var + LN_EPS) * g2_ref[...] + b2_ref[...]


def _combine(slots, yb, gates_t, h, g2, b2):
    t = h.shape[0]
    tm = TM_COMB
    return pl.pallas_call(
        _combine_kernel,
        grid=(t // tm,),
        in_specs=[
            pl.BlockSpec((TOP_K, tm), lambda i: (0, i), memory_space=pltpu.SMEM),
            pl.BlockSpec(memory_space=pl.ANY),
            pl.BlockSpec((tm, 8), lambda i: (i, 0)),
            pl.BlockSpec((tm, D_MODEL), lambda i: (i, 0)),
            pl.BlockSpec((1, D_MODEL), lambda i: (0, 0)),
            pl.BlockSpec((1, D_MODEL), lambda i: (0, 0)),
        ],
        out_specs=pl.BlockSpec((tm, D_MODEL), lambda i: (i, 0)),
        out_shape=jax.ShapeDtypeStruct((t, D_MODEL), F32),
        scratch_shapes=[pltpu.VMEM((TOP_K, tm, D_MODEL // 2), U32), pltpu.SemaphoreType.DMA],
        compiler_params=pltpu.CompilerParams(
            dimension_semantics=("arbitrary",), vmem_limit_bytes=VMEM_LIMIT),
        name="combine",
    )(slots, yb, gates_t, h, g2, b2)


def kernel(x, w_in, conv_w, p_a, hgrn_lb_logits, hgrn_norm_w, p_b, w_o, ln1_g, ln1_b,
           w_router, b_router, w1, b1, w2, b2, ln2_g, ln2_b):
    assert w_in.shape[0] == 1, "single-layer block"
    batch, seq, _ = x.shape
    t = batch * seq
    x2 = x.reshape(t, D_MODEL)

    (z, cb, v, g, ga, gb,
     qhf, khf, qtf, ktf, decf,
     qhb, khb, qtb, ktb, decb) = _inproj(
        x2, w_in[0].astype(BF16), hgrn_lb_logits.astype(F32).reshape(4, HDIM))

    yb = _hgrn((qhf, khf, qtf, ktf, decf), (qhb, khb, qtb, ktb, decb), v, g,
               hgrn_norm_w.astype(F32).reshape(1, HDIM), batch, seq)

    h, hp, er, gates, counts = _mix(
        z, cb, yb, ga, gb, x2, conv_w[0], p_a[0].astype(BF16), p_b[0].astype(BF16),
        w_o[0].astype(BF16), ln1_g.reshape(1, D_MODEL), ln1_b.reshape(1, D_MODEL),
        w_router[0].T.astype(BF16), b_router[0].reshape(N_EXPERTS, 1), seq)

    counts = counts[:, 0].astype(I32)
    padded = ((counts + MOE_BM - 1) // MOE_BM) * MOE_BM
    ends = jnp.cumsum(padded)
    pstart = ends - padded
    n_blocks = (t * TOP_K) // MOE_BM + N_EXPERTS
    slots = jnp.take(pstart, er[:TOP_K], axis=0) + er[TOP_K:]
    n_valid = (ends[-1] // MOE_BM).astype(I32)
    starts = jnp.minimum(jnp.arange(n_blocks, dtype=I32), n_valid - 1) * MOE_BM
    block_expert = jnp.clip(jnp.searchsorted(ends, starts, side="right"), 0, N_EXPERTS - 1).astype(I32)

    xb = _dispatch(slots, hp, n_blocks * MOE_BM)
    y_rows = _moe(block_expert, n_valid.reshape(1), xb, w1[0], b1[0][:, None, :], w2[0], b2[0][:, None, :])
    out = _combine(slots, y_rows, gates.T, h, ln2_g.reshape(1, D_MODEL), ln2_b.reshape(1, D_MODEL))
    return out.reshape(batch, seq, D_MODEL)
```
